```python
import jax
import jax.numpy as jnp
from jax import lax
import numpy as np


D_MODEL = 1024
BATCH = 4
SEQ = 8192
DEPTH = 1

PLE_DIM = 256
EXPAND = 2
D_MIX = EXPAND * D_MODEL
GLA_WIDTH = D_MIX // 2
GLA_HEADS = 4
GLA_KEY_WIDTH = GLA_WIDTH // 2
GLA_HEAD_K = GLA_KEY_WIDTH // GLA_HEADS
GLA_HEAD_V = GLA_WIDTH // GLA_HEADS
GLA_GATE_RANK = 16
GLA_GATE_NORMALIZER = 16.0
GLA_CHUNK = 64
SSD_WIDTH = D_MIX - GLA_WIDTH
SSD_HEAD_DIM = 64
SSD_HEADS = SSD_WIDTH // SSD_HEAD_DIM
SSD_GROUPS = 2
SSD_HEADS_PER_GROUP = SSD_HEADS // SSD_GROUPS
SSD_STATE = 128
SSD_CONV = 4
SSD_CHUNK = 64
SSD_CONV_DIM = SSD_WIDTH + 2 * SSD_GROUPS * SSD_STATE
EPS = 1e-6

SPLIT_SIZES = (GLA_KEY_WIDTH, GLA_KEY_WIDTH, GLA_WIDTH, GLA_WIDTH, GLA_GATE_RANK, SSD_WIDTH, SSD_CONV_DIM, SSD_HEADS)
D_IN_PROJ = sum(SPLIT_SIZES)

kernel_name = 'hymba_gla_ssd_hybrid_block'


def rmsnorm(x, w):
    xf = x.astype(jnp.float32)
    y = xf * lax.rsqrt(jnp.mean(xf * xf, axis=-1, keepdims=True) + EPS)
    return (y * w.astype(jnp.float32)).astype(x.dtype)


def split_columns(t, sizes):
    offs = np.cumsum(sizes)[:-1].tolist()
    return jnp.split(t, offs, axis=-1)


def causal_depthwise_conv(x, w, b):
    k_width = w.shape[0]
    t_len = x.shape[1]
    xp = jnp.pad(x, ((0, 0), (k_width - 1, 0), (0, 0)))
    out = b
    for j in range(k_width):
        out = out + xp[:, j:j + t_len] * w[j]
    return out


def gla_chunked(q, k, v, log_a):
    b_, t_, h_, dk = q.shape
    dv = v.shape[-1]
    n = t_ // GLA_CHUNK

    def to_chunks(a):
        return a.reshape(b_, n, GLA_CHUNK, h_, a.shape[-1]).transpose(1, 0, 3, 2, 4)

    causal = jnp.tril(jnp.ones((GLA_CHUNK, GLA_CHUNK), dtype=bool))

    def step(state, inp):
        qi, ki, vi, gi = inp
        b = jnp.cumsum(gi, axis=2)
        b_last = b[:, :, -1:, :]
        q_dec = qi * jnp.exp(b)
        scores = jnp.einsum('bhik,bhjk->bhij', q_dec, ki * jnp.exp(-b))
        scores = jnp.where(causal, scores, 0.0)
        o = jnp.einsum('bhij,bhjv->bhiv', scores, vi) + jnp.einsum('bhik,bhkv->bhiv', q_dec, state)
        k_dec = ki * jnp.exp(b_last - b)
        state = state * jnp.exp(b_last[:, :, 0, :])[..., None] + jnp.einsum('bhjk,bhjv->bhkv', k_dec, vi)
        return state, o

    state0 = jnp.zeros((b_, h_, dk, dv), q.dtype)
    _, o = lax.scan(step, state0, (to_chunks(q * dk ** -0.5), to_chunks(k), to_chunks(v), to_chunks(log_a)))
    return o.transpose(1, 0, 3, 2, 4).reshape(b_, t_, h_, dv)


def ssd_chunked(x, dt, a_neg, bm, cm):
    b_, t_, h_, p_ = x.shape
    n = t_ // SSD_CHUNK
    g_, hg, ns = SSD_GROUPS, SSD_HEADS_PER_GROUP, SSD_STATE
    xc = x.reshape(b_, n, SSD_CHUNK, g_, hg, p_).transpose(1, 0, 2, 3, 4, 5)
    dtc = dt.reshape(b_, n, SSD_CHUNK, g_, hg).transpose(1, 0, 2, 3, 4)
    bc = bm.reshape(b_, n, SSD_CHUNK, g_, ns).transpose(1, 0, 2, 3, 4)
    cc = cm.reshape(b_, n, SSD_CHUNK, g_, ns).transpose(1, 0, 2, 3, 4)
    a_g = a_neg.reshape(g_, hg)
    causal = jnp.tril(jnp.ones((SSD_CHUNK, SSD_CHUNK), dtype=bool))[None, :, :, None, None]

    def step(state, inp):
        xi, dti, bi, ci = inp
        cum = jnp.cumsum(dti * a_g, axis=1)
        seg = cum[:, :, None] - cum[:, None, :]
        decay_ij = jnp.where(causal, jnp.exp(jnp.minimum(seg, 0.0)), 0.0)
        cb = jnp.einsum('bign,bjgn->bijg', ci, bi)
        xdt = xi * dti[..., None]
        y = jnp.einsum('bijgh,bjghp->bighp', cb[..., None] * decay_ij, xdt)
        y = y + jnp.einsum('bign,bghpn->bighp', ci, state) * jnp.exp(cum)[..., None]
        decay_to_end = jnp.exp(cum[:, -1:] - cum)
        state = state * jnp.exp(cum[:, -1])[..., None, None] + jnp.einsum('bjgn,bjghp->bghpn', bi, xdt * decay_to_end[..., None])
        return state, y

    state0 = jnp.zeros((b_, g_, hg, p_, ns), x.dtype)
    _, y = lax.scan(step, state0, (xc, dtc, bc, cc))
    return y.transpose(1, 0, 2, 3, 4, 5).reshape(b_, t_, h_, p_)


def hybrid_layer(h, p_i, norm_w, w_in, gla_gate_up, gla_gate_b, gla_norm_w, conv_w, conv_b,
                 dt_bias, a_log, d_skip, ssd_norm_w, w_out, w_pe, w_pe_gate, pe_norm_w):
    b_, t_, _ = h.shape
    f32 = jnp.float32
    u = rmsnorm(h, norm_w)
    proj = jnp.matmul(u, w_in).astype(f32)
    q, k, v, g, gate_lr, z, xbc, dt_raw = split_columns(proj, SPLIT_SIZES)

    log_a = jax.nn.log_sigmoid(jnp.matmul(gate_lr, gla_gate_up.astype(f32)) + gla_gate_b.astype(f32)) / GLA_GATE_NORMALIZER
    heads_k = lambda a: a.reshape(b_, t_, GLA_HEADS, GLA_HEAD_K)
    o = gla_chunked(heads_k(q), heads_k(k), v.reshape(b_, t_, GLA_HEADS, GLA_HEAD_V), heads_k(log_a))
    gla_out = rmsnorm(o, gla_norm_w).reshape(b_, t_, GLA_WIDTH) * jax.nn.silu(g)

    xbc = jax.nn.silu(causal_depthwise_conv(xbc, conv_w.astype(f32), conv_b.astype(f32)))
    xs, bm, cm = split_columns(xbc, (SSD_WIDTH, SSD_GROUPS * SSD_STATE, SSD_GROUPS * SSD_STATE))
    dt = jax.nn.softplus(dt_raw + dt_bias.astype(f32))
    a_neg = -jnp.exp(a_log.astype(f32))
    xs_h = xs.reshape(b_, t_, SSD_HEADS, SSD_HEAD_DIM)
    y = ssd_chunked(xs_h, dt, a_neg, bm.reshape(b_, t_, SSD_GROUPS, SSD_STATE), cm.reshape(b_, t_, SSD_GROUPS, SSD_STATE))
    y = (y + xs_h * d_skip.astype(f32)[:, None]).reshape(b_, t_, SSD_WIDTH) * jax.nn.silu(z)
    y = rmsnorm(y.reshape(b_, t_, SSD_GROUPS, SSD_WIDTH // SSD_GROUPS), ssd_norm_w.reshape(SSD_GROUPS, -1)).reshape(b_, t_, SSD_WIDTH)

    mixed = jnp.concatenate([gla_out, y], axis=-1)
    h = h + jnp.matmul(mixed, w_out.astype(f32)).astype(h.dtype)

    gate = jax.nn.sigmoid(jnp.matmul(rmsnorm(h, pe_norm_w), w_pe_gate).astype(f32))
    h = h + (gate * jnp.matmul(p_i, w_pe).astype(f32)).astype(h.dtype)
    return h


def setup_inputs(seed: int = 0) -> dict:
    key = jax.random.key(seed)
    ks = jax.random.split(key, 20)
    f32 = jnp.float32
    nrm = lambda k, shape, s: jax.random.normal(k, shape, f32) * s
    x = jax.random.normal(ks[0], (BATCH, SEQ, D_MODEL), f32)
    p = jax.random.normal(ks[1], (DEPTH, BATCH, SEQ, PLE_DIM), f32)
    norm_w = 1.0 + nrm(ks[2], (DEPTH, D_MODEL), 0.02)
    w_in = nrm(ks[3], (DEPTH, D_MODEL, D_IN_PROJ), D_MODEL ** -0.5)
    gla_gate_up = nrm(ks[4], (DEPTH, GLA_GATE_RANK, GLA_KEY_WIDTH), GLA_GATE_RANK ** -0.5)
    gla_gate_b = nrm(ks[5], (DEPTH, GLA_KEY_WIDTH), 0.1)
    gla_norm_w = 1.0 + nrm(ks[6], (DEPTH, GLA_HEAD_V), 0.02)
    conv_w = nrm(ks[7], (DEPTH, SSD_CONV, SSD_CONV_DIM), 0.5)
    conv_b = nrm(ks[8], (DEPTH, SSD_CONV_DIM), 0.02)
    dt0 = jnp.exp(jax.random.uniform(ks[9], (DEPTH, SSD_HEADS), f32) * (np.log(0.1) - np.log(0.001)) + np.log(0.001))
    dt_bias = dt0 + jnp.log(-jnp.expm1(-dt0))
    a_log = jnp.log(jax.random.uniform(ks[10], (DEPTH, SSD_HEADS), f32, 1.0, 16.0))
    d_skip = 1.0 + nrm(ks[11], (DEPTH, SSD_HEADS), 0.1)
    ssd_norm_w = 1.0 + nrm(ks[12], (DEPTH, SSD_WIDTH), 0.02)
    w_out = nrm(ks[13], (DEPTH, D_MIX, D_MODEL), D_MIX ** -0.5)
    w_pe = nrm(ks[14], (DEPTH, PLE_DIM, D_MODEL), PLE_DIM ** -0.5)
    w_pe_gate = nrm(ks[15], (DEPTH, D_MODEL, D_MODEL), D_MODEL ** -0.5)
    pe_norm_w = 1.0 + nrm(ks[16], (DEPTH, D_MODEL), 0.02)
    final_norm_w = 1.0 + nrm(ks[17], (D_MODEL,), 0.02)
    return {'x': x, 'p': p, 'norm_w': norm_w, 'w_in': w_in, 'gla_gate_up': gla_gate_up,
            'gla_gate_b': gla_gate_b, 'gla_norm_w': gla_norm_w, 'conv_w': conv_w, 'conv_b': conv_b,
            'dt_bias': dt_bias, 'a_log': a_log, 'd_skip': d_skip, 'ssd_norm_w': ssd_norm_w,
            'w_out': w_out, 'w_pe': w_pe, 'w_pe_gate': w_pe_gate, 'pe_norm_w': pe_norm_w,
            'final_norm_w': final_norm_w}


def reference(x, p, norm_w, w_in, gla_gate_up, gla_gate_b, gla_norm_w, conv_w, conv_b,
              dt_bias, a_log, d_skip, ssd_norm_w, w_out, w_pe, w_pe_gate, pe_norm_w, final_norm_w):
    h = x
    for i in range(DEPTH):
        h = hybrid_layer(h, p[i], norm_w[i], w_in[i], gla_gate_up[i], gla_gate_b[i], gla_norm_w[i],
                         conv_w[i], conv_b[i], dt_bias[i], a_log[i], d_skip[i], ssd_norm_w[i],
                         w_out[i], w_pe[i], w_pe_gate[i], pe_norm_w[i])
    return rmsnorm(h, final_norm_w)
```

```python
import functools

import jax
import jax.numpy as jnp
import numpy as np
from jax import lax
from jax.experimental import pallas as pl
from jax.experimental.pallas import tpu as pltpu

F32 = jnp.float32
BF16 = jnp.bfloat16

EPS = 1e-6
CHUNK = 64
GLA_HEADS = 4
GLA_HEAD_K = 128
GLA_HEAD_V = 256
GLA_KEY_WIDTH = GLA_HEADS * GLA_HEAD_K
GLA_WIDTH = GLA_HEADS * GLA_HEAD_V
GLA_GATE_RANK = 16
GLA_GATE_NORMALIZER = 16.0
SSD_HEADS = 16
SSD_HEAD_DIM = 64
SSD_WIDTH = SSD_HEADS * SSD_HEAD_DIM
SSD_GROUPS = 2
SSD_HEADS_PER_GROUP = SSD_HEADS // SSD_GROUPS
SSD_GROUP_WIDTH = SSD_WIDTH // SSD_GROUPS
SSD_STATE = 128
SSD_CONV = 4
SSD_BC_WIDTH = SSD_GROUPS * SSD_STATE
SSD_CONV_DIM = SSD_WIDTH + 2 * SSD_BC_WIDTH
LANES = 128
SUBLANES = 8
SMALL_WIDTH = LANES
DT_LANE0 = GLA_GATE_RANK

IN_PROJ_ROWS = 256
GLA_ROWS = 512
SSD_ROWS = 256
OUT_PROJ_ROWS = 512
VMEM_LIMIT = 56 * 1024 * 1024


def _rmsnorm(x, w):
    return x * lax.rsqrt(jnp.mean(x * x, axis=-1, keepdims=True) + EPS) * w


def _dot(a, b):
    return jnp.dot(a, b, preferred_element_type=F32)


def _dot_nt(a, b):
    return lax.dot_general(a, b, (((1,), (1,)), ((), ())), preferred_element_type=F32)


def _dot_tn(a, b):
    return lax.dot_general(a, b, (((0,), (0,)), ((), ())), preferred_element_type=F32)


def _split2(x):
    hi = x.astype(BF16)
    lo = (x - hi.astype(F32)).astype(BF16)
    return hi, lo


def _chunk_cumsum(x, tri2):
    hi, lo = _split2(x)
    return _dot(tri2, jnp.concatenate([hi, lo], axis=0))


def _tri2():
    r = lax.broadcasted_iota(jnp.int32, (CHUNK, 2 * CHUNK), 0)
    c = lax.broadcasted_iota(jnp.int32, (CHUNK, 2 * CHUNK), 1)
    c = jnp.where(c >= CHUNK, c - CHUNK, c)
    return jnp.where(r >= c, 1.0, 0.0).astype(BF16)


def _causal():
    r = lax.broadcasted_iota(jnp.int32, (CHUNK, CHUNK), 0)
    c = lax.broadcasted_iota(jnp.int32, (CHUNK, CHUNK), 1)
    return r >= c


def _in_proj_kernel(x_ref, nw_ref, wq_ref, wk_ref, wv_ref, wg_ref, wz_ref, wx_ref, ws_ref,
                    q_ref, k_ref, v_ref, g_ref, z_ref, xbc_ref, s_ref):
    u = _rmsnorm(x_ref[...], nw_ref[...]).astype(BF16)
    q_ref[...] = _dot(u, wq_ref[...])
    k_ref[...] = _dot(u, wk_ref[...])
    v_ref[...] = _dot(u, wv_ref[...]).astype(BF16)
    g_ref[...] = _dot(u, wg_ref[...])
    z_ref[...] = _dot(u, wz_ref[...])
    xbc_ref[...] = _dot(u, wx_ref[...])
    s_ref[...] = _dot(u, ws_ref[...])


def _in_proj(x2, norm_w, weights):
    n, d = x2.shape
    tm = IN_PROJ_ROWS
    widths = [w.shape[1] for w in weights]
    dtypes = [F32, F32, BF16, F32, F32, F32, F32]
    row = lambda i: (i, 0)
    fixed = lambda i: (0, 0)
    in_specs = [pl.BlockSpec((tm, d), row), pl.BlockSpec((1, d), fixed)]
    in_specs += [pl.BlockSpec((d, w), fixed) for w in widths]
    out_specs = [pl.BlockSpec((tm, w), row) for w in widths]
    out_shape = [jax.ShapeDtypeStruct((n, w), dt) for w, dt in zip(widths, dtypes)]
    return pl.pallas_call(
        _in_proj_kernel,
        grid=(n // tm,),
        in_specs=in_specs,
        out_specs=out_specs,
        out_shape=out_shape,
        compiler_params=pltpu.CompilerParams(
            dimension_semantics=("arbitrary",), vmem_limit_bytes=VMEM_LIMIT),
        name="in_proj",
    )(x2, norm_w, *weights)


def _gla_kernel(q_ref, k_ref, v_ref, g_ref, s_ref, up_ref, gb_ref, nw_ref, o_ref, st_ref):
    @pl.when(pl.program_id(2) == 0)
    def _():
        st_ref[...] = jnp.zeros_like(st_ref)

    rows = q_ref.shape[0]
    tri2 = _tri2()
    causal = _causal()
    scale = GLA_HEAD_K ** -0.5

    gate = _dot(s_ref[...].astype(BF16), up_ref[...]) + gb_ref[...]
    log_a = jax.nn.log_sigmoid(gate) * (1.0 / GLA_GATE_NORMALIZER)
    nw = nw_ref[...]

    state = st_ref[...]
    for c in range(rows // CHUNK):
        sl = pl.ds(c * CHUNK, CHUNK)
        b = _chunk_cumsum(log_a[c * CHUNK:(c + 1) * CHUNK], tri2)
        b_last = b[CHUNK - 1:CHUNK, :]
        q_c = q_ref[sl, :] * scale
        k_c = k_ref[sl, :]
        v_c = v_ref[sl, :]
        q_dec = (q_c * jnp.exp(b)).astype(BF16)
        k_inv = (k_c * jnp.exp(-b)).astype(BF16)
        k_dec = (k_c * jnp.exp(b_last - b)).astype(BF16)
        scores = jnp.where(causal, _dot_nt(q_dec, k_inv), 0.0)
        o = _dot(scores.astype(BF16), v_c) + _dot_nt(q_dec, state.astype(BF16))
        state = state * jnp.exp(b_last) + _dot_tn(v_c, k_dec)
        g_c = g_ref[sl, :]
        o_ref[sl, :] = (_rmsnorm(o, nw) * jax.nn.silu(g_c)).astype(o_ref.dtype)
    st_ref[...] = state


def _gla(q, k, v, g, small, up_pad, gate_b, norm_w):
    bsz, t, _ = q.shape
    tb = GLA_ROWS
    grid = (bsz, GLA_HEADS, t // tb)
    head = lambda w: pl.BlockSpec((None, tb, w), lambda b, h, i: (b, i, h))
    in_specs = [
        head(GLA_HEAD_K), head(GLA_HEAD_K), head(GLA_HEAD_V), head(GLA_HEAD_V),
        pl.BlockSpec((None, tb, SMALL_WIDTH), lambda b, h, i: (b, i, 0)),
        pl.BlockSpec((SMALL_WIDTH, GLA_HEAD_K), lambda b, h, i: (0, h)),
        pl.BlockSpec((1, GLA_HEAD_K), lambda b, h, i: (0, h)),
        pl.BlockSpec((1, GLA_HEAD_V), lambda b, h, i: (0, 0)),
    ]
    return pl.pallas_call(
        _gla_kernel,
        grid=grid,
        in_specs=in_specs,
        out_specs=head(GLA_HEAD_V),
        out_shape=jax.ShapeDtypeStruct((bsz, t, GLA_WIDTH), BF16),
        scratch_shapes=[pltpu.VMEM((GLA_HEAD_V, GLA_HEAD_K), F32)],
        compiler_params=pltpu.CompilerParams(
            dimension_semantics=("arbitrary", "arbitrary", "arbitrary"),
            vmem_limit_bytes=VMEM_LIMIT),
        name="gla",
    )(q, k, v, g, small, up_pad, gate_b, norm_w)


def _ssd_kernel(z_ref, xbc_ref, s_ref, cw_ref, cb_ref, dtb_ref, alog_ref, e2_ref, dskip_ref,
                nw_ref, o_ref, xpad_ref, st_ref):
    rows = xbc_ref.shape[0]

    @pl.when(pl.program_id(1) == 0)
    def _():
        xpad_ref[0:SUBLANES, :] = jnp.zeros((SUBLANES, SSD_CONV_DIM), F32)
        st_ref[...] = jnp.zeros_like(st_ref)

    xpad_ref[SUBLANES:SUBLANES + rows, :] = xbc_ref[...]
    acc = cb_ref[...] + xpad_ref[SUBLANES:SUBLANES + rows, :] * cw_ref[SSD_CONV - 1:SSD_CONV, :]
    for j in range(SSD_CONV - 1):
        off = SUBLANES - (SSD_CONV - 1) + j
        acc = acc + xpad_ref[off:off + rows, :] * cw_ref[j:j + 1, :]
    xpad_ref[0:SUBLANES, :] = xpad_ref[rows:rows + SUBLANES, :]
    xbc = jax.nn.silu(acc)

    lane = lax.broadcasted_iota(jnp.int32, (1, SMALL_WIDTH), 1)
    is_dt = (lane >= DT_LANE0) & (lane < DT_LANE0 + SSD_HEADS)
    dt = jnp.where(is_dt, jax.nn.softplus(s_ref[...] + dtb_ref[...]), 0.0)
    da = dt * jnp.where(is_dt, -jnp.exp(alog_ref[...]), 0.0)

    tri2 = _tri2()
    causal = _causal()
    e2 = e2_ref[...]

    def expand(x):
        hi, lo = _split2(x)
        return _dot(jnp.concatenate([hi, lo], axis=1), e2)

    dt_e_all = expand(dt)
    dskip = dskip_ref[...]
    nw = nw_ref[...]

    for c in range(rows // CHUNK):
        lo_r, hi_r = c * CHUNK, (c + 1) * CHUNK
        cum = _chunk_cumsum(da[lo_r:hi_r], tri2)
        cum_e = expand(cum)
        dt_e = dt_e_all[lo_r:hi_r]
        cum_last = cum_e[CHUNK - 1:CHUNK, :]
        decay_in = jnp.exp(cum_e)
        decay_last = jnp.exp(cum_last)
        xs = xbc[lo_r:hi_r, :SSD_WIDTH]
        xdt = xs * dt_e
        xdt_b = xdt.astype(BF16)
        xd_end = (xdt * jnp.exp(cum_last - cum_e)).astype(BF16)
        ys = []
        for g in range(SSD_GROUPS):
            b_g = xbc[lo_r:hi_r, SSD_WIDTH + g * SSD_STATE:SSD_WIDTH + (g + 1) * SSD_STATE].astype(BF16)
            c_lo = SSD_WIDTH + SSD_BC_WIDTH + g * SSD_STATE
            c_g = xbc[lo_r:hi_r, c_lo:c_lo + SSD_STATE].astype(BF16)
            cb = _dot_nt(c_g, b_g)
            gs = slice(g * SSD_GROUP_WIDTH, (g + 1) * SSD_GROUP_WIDTH)
            state = st_ref[g]
            y_g = _dot(c_g, state.astype(BF16)) * decay_in[:, gs]
            intra = []
            for hh in range(SSD_HEADS_PER_GROUP):
                h = g * SSD_HEADS_PER_GROUP + hh
                hs = slice(h * SSD_HEAD_DIM, (h + 1) * SSD_HEAD_DIM)
                col = cum_e[:, hs]
                seg = col - col.T
                l_h = jnp.where(causal, cb * jnp.exp(jnp.minimum(seg, 0.0)), 0.0)
                intra.append(_dot(l_h.astype(BF16), xdt_b[:, hs]))
            ys.append(y_g + jnp.concatenate(intra, axis=1))
            st_ref[g] = state * decay_last[:, gs] + _dot_tn(b_g, xd_end[:, gs])
        y = jnp.concatenate(ys, axis=1) + xs * dskip
        y = y * jax.nn.silu(z_ref[lo_r:hi_r, :])
        outs = []
        for g in range(SSD_GROUPS):
            gs = slice(g * SSD_GROUP_WIDTH, (g + 1) * SSD_GROUP_WIDTH)
            outs.append(_rmsnorm(y[:, gs], nw[:, gs]))
        o_ref[lo_r:hi_r, :] = jnp.concatenate(outs, axis=1).astype(o_ref.dtype)


def _ssd(z, xbc, small, conv_w, conv_b, dtb_pad, alog_pad, e2, dskip_e, norm_w):
    bsz, t, _ = z.shape
    tb = SSD_ROWS
    grid = (bsz, t // tb)
    blk = lambda w: pl.BlockSpec((None, tb, w), lambda b, i: (b, i, 0))
    fixed = lambda shape: pl.BlockSpec(shape, lambda b, i: (0,) * len(shape))
    in_specs = [
        blk(SSD_WIDTH), blk(SSD_CONV_DIM), blk(SMALL_WIDTH),
        fixed(conv_w.shape), fixed(conv_b.shape), fixed(dtb_pad.shape), fixed(alog_pad.shape),
        fixed(e2.shape), fixed(dskip_e.shape), fixed(norm_w.shape),
    ]
    return pl.pallas_call(
        _ssd_kernel,
        grid=grid,
        in_specs=in_specs,
        out_specs=blk(SSD_WIDTH),
        out_shape=jax.ShapeDtypeStruct((bsz, t, SSD_WIDTH), BF16),
        scratch_shapes=[
            pltpu.VMEM((tb + SUBLANES, SSD_CONV_DIM), F32),
            pltpu.VMEM((SSD_GROUPS, SSD_STATE, SSD_GROUP_WIDTH), F32),
        ],
        compiler_params=pltpu.CompilerParams(
            dimension_semantics=("arbitrary", "arbitrary"), vmem_limit_bytes=VMEM_LIMIT),
        name="ssd",
    )(z, xbc, small, conv_w, conv_b, dtb_pad, alog_pad, e2, dskip_e, norm_w)


def _out_proj_kernel(x_ref, gla_ref, ssd_ref, p_ref, wo1_ref, wo2_ref, wpe_ref, wgate_ref,
                     pnw_ref, fnw_ref, o_ref, *, final):
    h = x_ref[...] + _dot(gla_ref[...], wo1_ref[...]) + _dot(ssd_ref[...], wo2_ref[...])
    hn = _rmsnorm(h, pnw_ref[...]).astype(BF16)
    gate = jax.nn.sigmoid(_dot(hn, wgate_ref[...]))
    h = h + gate * _dot(p_ref[...].astype(BF16), wpe_ref[...])
    if final:
        h = _rmsnorm(h, fnw_ref[...])
    o_ref[...] = h


def _out_proj(x2, gla, ssd, p2, wo1, wo2, wpe, wgate, pe_norm_w, final_norm_w, final):
    n, d = x2.shape
    tm = OUT_PROJ_ROWS
    row = lambda w: pl.BlockSpec((tm, w), lambda i: (i, 0))
    fixed = lambda a: pl.BlockSpec(a.shape, lambda i: (0, 0))
    in_specs = [row(d), row(gla.shape[1]), row(ssd.shape[1]), row(p2.shape[1]),
                fixed(wo1), fixed(wo2), fixed(wpe), fixed(wgate), fixed(pe_norm_w), fixed(final_norm_w)]
    return pl.pallas_call(
        functools.partial(_out_proj_kernel, final=final),
        grid=(n // tm,),
        in_specs=in_specs,
        out_specs=row(d),
        out_shape=jax.ShapeDtypeStruct((n, d), F32),
        compiler_params=pltpu.CompilerParams(
            dimension_semantics=("arbitrary",), vmem_limit_bytes=VMEM_LIMIT),
        name="out_proj",
    )(x2, gla, ssd, p2, wo1, wo2, wpe, wgate, pe_norm_w, final_norm_w)


def _expansion_matrix():
    e = np.zeros((2 * SMALL_WIDTH, SSD_WIDTH), np.float32)
    for h in range(SSD_HEADS):
        e[DT_LANE0 + h, h * SSD_HEAD_DIM:(h + 1) * SSD_HEAD_DIM] = 1.0
        e[SMALL_WIDTH + DT_LANE0 + h, h * SSD_HEAD_DIM:(h + 1) * SSD_HEAD_DIM] = 1.0
    return jnp.asarray(e, BF16)


def _pad_small(v):
    return jnp.zeros((1, SMALL_WIDTH), F32).at[0, DT_LANE0:DT_LANE0 + SSD_HEADS].set(v.astype(F32))


def _layer(h2, p2, bsz, t, final, norm_w, w_in, gla_gate_up, gla_gate_b, gla_norm_w, conv_w, conv_b,
           dt_bias, a_log, d_skip, ssd_norm_w, w_out, w_pe, w_pe_gate, pe_norm_w, final_norm_w):
    d = h2.shape[1]
    sizes = (GLA_KEY_WIDTH, GLA_KEY_WIDTH, GLA_WIDTH, GLA_WIDTH, GLA_GATE_RANK, SSD_WIDTH,
             SSD_CONV_DIM, SSD_HEADS)
    offs = np.concatenate([[0], np.cumsum(sizes)])
    wq, wk, wv, wg, wlr, wz, wx, wdt = [w_in[:, offs[i]:offs[i + 1]] for i in range(len(sizes))]
    pad = jnp.zeros((d, SMALL_WIDTH - GLA_GATE_RANK - SSD_HEADS), w_in.dtype)
    wsmall = jnp.concatenate([wlr, wdt, pad], axis=1)
    weights = [w.astype(BF16) for w in (wq, wk, wv, wg, wz, wx, wsmall)]

    q, k, v, g, z, xbc, small = _in_proj(h2, norm_w.reshape(1, d).astype(F32), weights)
    r3 = lambda a: a.reshape(bsz, t, a.shape[1])

    up_pad = jnp.zeros((SMALL_WIDTH, GLA_KEY_WIDTH), BF16).at[:GLA_GATE_RANK].set(gla_gate_up.astype(BF16))
    gla = _gla(r3(q), r3(k), r3(v), r3(g), r3(small), up_pad,
               gla_gate_b.reshape(1, -1).astype(F32), gla_norm_w.reshape(1, -1).astype(F32))

    ssd = _ssd(r3(z), r3(xbc), r3(small), conv_w.astype(F32), conv_b.reshape(1, -1).astype(F32),
               _pad_small(dt_bias), _pad_small(a_log), _expansion_matrix(),
               jnp.repeat(d_skip.astype(F32), SSD_HEAD_DIM).reshape(1, -1),
               ssd_norm_w.reshape(1, -1).astype(F32))

    n = bsz * t
    return _out_proj(h2, gla.reshape(n, GLA_WIDTH), ssd.reshape(n, SSD_WIDTH), p2,
                     w_out[:GLA_WIDTH].astype(BF16), w_out[GLA_WIDTH:].astype(BF16),
                     w_pe.astype(BF16), w_pe_gate.astype(BF16),
                     pe_norm_w.reshape(1, d).astype(F32), final_norm_w.reshape(1, d).astype(F32), final)


def kernel(x, p, norm_w, w_in, gla_gate_up, gla_gate_b, gla_norm_w, conv_w, conv_b, dt_bias, a_log,
           d_skip, ssd_norm_w, w_out, w_pe, w_pe_gate, pe_norm_w, final_norm_w):
    bsz, t, d = x.shape
    depth = p.shape[0]
    h2 = x.reshape(bsz * t, d)
    for i in range(depth):
        h2 = _layer(h2, p[i].reshape(bsz * t, -1), bsz, t, i == depth - 1,
                    norm_w[i], w_in[i], gla_gate_up[i], gla_gate_b[i], gla_norm_w[i], conv_w[i],
                    conv_b[i], dt_bias[i], a_log[i], d_skip[i], ssd_norm_w[i], w_out[i], w_pe[i],
                    w_pe_gate[i], pe_norm_w[i], final_norm_w)
    return h2.reshape(bsz, t, d)
```

```python
import functools

import jax
import jax.numpy as jnp
import numpy as np
from jax import lax
from jax.experimental import pallas as pl
from jax.experimental.pallas import tpu as pltpu

F32 = jnp.float32
BF16 = jnp.bfloat16

EPS = 1e-6
CHUNK = 64
GLA_HEADS = 4
GLA_HEAD_K = 128
GLA_HEAD_V = 256
GLA_KEY_WIDTH = GLA_HEADS * GLA_HEAD_K
GLA_WIDTH = GLA_HEADS * GLA_HEAD_V
GLA_GATE_RANK = 16
GLA_GATE_NORMALIZER = 16.0
SSD_HEADS = 16
SSD_HEAD_DIM = 64
SSD_WIDTH = SSD_HEADS * SSD_HEAD_DIM
SSD_GROUPS = 2
SSD_HEADS_PER_GROUP = SSD_HEADS // SSD_GROUPS
SSD_GROUP_WIDTH = SSD_WIDTH // SSD_GROUPS
SSD_STATE = 128
SSD_CONV = 4
SSD_BC_WIDTH = SSD_GROUPS * SSD_STATE
SSD_CONV_DIM = SSD_WIDTH + 2 * SSD_BC_WIDTH
LANES = 128
SUBLANES = 8
SMALL_WIDTH = LANES
DT_LANE0 = GLA_GATE_RANK

IN_PROJ_ROWS = 256
GLA_ROWS = 512
SSD_ROWS = 256
OUT_PROJ_ROWS = 512
VMEM_LIMIT = 56 * 1024 * 1024


def _rmsnorm(x, w):
    return x * lax.rsqrt(jnp.mean(x * x, axis=-1, keepdims=True) + EPS) * w


def _dot(a, b):
    return jnp.dot(a, b, preferred_element_type=F32)


def _dot_nt(a, b):
    return lax.dot_general(a, b, (((1,), (1,)), ((), ())), preferred_element_type=F32)


def _dot_tn(a, b):
    return lax.dot_general(a, b, (((0,), (0,)), ((), ())), preferred_element_type=F32)


def _split2(x):
    hi = x.astype(BF16)
    lo = (x - hi.astype(F32)).astype(BF16)
    return hi, lo


def _chunk_cumsum(x, tri2):
    hi, lo = _split2(x)
    return _dot(tri2, jnp.concatenate([hi, lo], axis=0))


def _tri2():
    r = lax.broadcasted_iota(jnp.int32, (CHUNK, 2 * CHUNK), 0)
    c = lax.broadcasted_iota(jnp.int32, (CHUNK, 2 * CHUNK), 1)
    c = jnp.where(c >= CHUNK, c - CHUNK, c)
    return jnp.where(r >= c, 1.0, 0.0).astype(BF16)


def _causal():
    r = lax.broadcasted_iota(jnp.int32, (CHUNK, CHUNK), 0)
    c = lax.broadcasted_iota(jnp.int32, (CHUNK, CHUNK), 1)
    return r >= c


def _in_proj_kernel(x_ref, nw_ref, wq_ref, wk_ref, wv_ref, wg_ref, wz_ref, wx_ref, ws_ref,
                    q_ref, k_ref, v_ref, g_ref, z_ref, xbc_ref, s_ref):
    u = _rmsnorm(x_ref[...], nw_ref[...]).astype(BF16)
    q_ref[...] = _dot(u, wq_ref[...])
    k_ref[...] = _dot(u, wk_ref[...])
    v_ref[...] = _dot(u, wv_ref[...]).astype(BF16)
    g_ref[...] = _dot(u, wg_ref[...])
    z_ref[...] = _dot(u, wz_ref[...])
    xbc_ref[...] = _dot(u, wx_ref[...])
    s_ref[...] = _dot(u, ws_ref[...])


def _in_proj(x2, norm_w, weights):
    n, d = x2.shape
    tm = IN_PROJ_ROWS
    widths = [w.shape[1] for w in weights]
    dtypes = [F32, F32, BF16, F32, F32, F32, F32]
    row = lambda i: (i, 0)
    fixed = lambda i: (0, 0)
    in_specs = [pl.BlockSpec((tm, d), row), pl.BlockSpec((1, d), fixed)]
    in_specs += [pl.BlockSpec((d, w), fixed) for w in widths]
    out_specs = [pl.BlockSpec((tm, w), row) for w in widths]
    out_shape = [jax.ShapeDtypeStruct((n, w), dt) for w, dt in zip(widths, dtypes)]
    return pl.pallas_call(
        _in_proj_kernel,
        grid=(n // tm,),
        in_specs=in_specs,
        out_specs=out_specs,
        out_shape=out_shape,
        compiler_params=pltpu.CompilerParams(
            dimension_semantics=("arbitrary",), vmem_limit_bytes=VMEM_LIMIT),
        name="in_proj",
    )(x2, norm_w, *weights)


def _gla_kernel(q_ref, k_ref, v_ref, g_ref, s_ref, up_ref, gb_ref, nw_ref, o_ref, st_ref):
    @pl.when(pl.program_id(2) == 0)
    def _():
        st_ref[...] = jnp.zeros_like(st_ref)

    rows = q_ref.shape[0]
    chunks = range(rows // CHUNK)
    rsl = lambda c: pl.ds(c * CHUNK, CHUNK)
    causal = _causal()
    scale = GLA_HEAD_K ** -0.5

    gate = _dot(s_ref[...].astype(BF16), up_ref[...]) + gb_ref[...]
    log_a = jax.nn.log_sigmoid(gate) * (1.0 / GLA_GATE_NORMALIZER)

    la = jnp.concatenate([log_a[c * CHUNK:(c + 1) * CHUNK] for c in chunks], axis=1)
    b_all = _chunk_cumsum(la, _tri2())

    q_dec, k_inv, k_dec, d_last = [], [], [], []
    for c in chunks:
        b = b_all[:, c * GLA_HEAD_K:(c + 1) * GLA_HEAD_K]
        b_last = b[CHUNK - 1:CHUNK, :]
        q_c = q_ref[rsl(c), :] * scale
        k_c = k_ref[rsl(c), :]
        q_dec.append((q_c * jnp.exp(b)).astype(BF16))
        k_inv.append((k_c * jnp.exp(-b)).astype(BF16))
        k_dec.append((k_c * jnp.exp(b_last - b)).astype(BF16))
        d_last.append(jnp.exp(b_last))

    scores = [jnp.where(causal, _dot_nt(q_dec[c], k_inv[c]), 0.0).astype(BF16) for c in chunks]
    upd = [_dot_tn(v_ref[rsl(c), :], k_dec[c]) for c in chunks]

    state = st_ref[...]
    states = []
    for c in chunks:
        states.append(state.astype(BF16))
        state = state * d_last[c] + upd[c]
    st_ref[...] = state

    nw = nw_ref[...]
    for c in chunks:
        o = _dot(scores[c], v_ref[rsl(c), :]) + _dot_nt(q_dec[c], states[c])
        g_c = g_ref[rsl(c), :]
        o_ref[rsl(c), :] = (_rmsnorm(o, nw) * jax.nn.silu(g_c)).astype(o_ref.dtype)


def _gla(q, k, v, g, small, up_pad, gate_b, norm_w):
    bsz, t, _ = q.shape
    tb = GLA_ROWS
    grid = (bsz, GLA_HEADS, t // tb)
    head = lambda w: pl.BlockSpec((None, tb, w), lambda b, h, i: (b, i, h))
    in_specs = [
        head(GLA_HEAD_K), head(GLA_HEAD_K), head(GLA_HEAD_V), head(GLA_HEAD_V),
        pl.BlockSpec((None, tb, SMALL_WIDTH), lambda b, h, i: (b, i, 0)),
        pl.BlockSpec((SMALL_WIDTH, GLA_HEAD_K), lambda b, h, i: (0, h)),
        pl.BlockSpec((1, GLA_HEAD_K), lambda b, h, i: (0, h)),
        pl.BlockSpec((1, GLA_HEAD_V), lambda b, h, i: (0, 0)),
    ]
    return pl.pallas_call(
        _gla_kernel,
        grid=grid,
        in_specs=in_specs,
        out_specs=head(GLA_HEAD_V),
        out_shape=jax.ShapeDtypeStruct((bsz, t, GLA_WIDTH), BF16),
        scratch_shapes=[pltpu.VMEM((GLA_HEAD_V, GLA_HEAD_K), F32)],
        compiler_params=pltpu.CompilerParams(
            dimension_semantics=("arbitrary", "arbitrary", "arbitrary"),
            vmem_limit_bytes=VMEM_LIMIT),
        name="gla",
    )(q, k, v, g, small, up_pad, gate_b, norm_w)


def _ssd_kernel(z_ref, xbc_ref, s_ref, cw_ref, cb_ref, dtb_ref, alog_ref, e2_ref, dskip_ref,
                nw_ref, o_ref, xpad_ref, st_ref):
    rows = xbc_ref.shape[0]

    @pl.when(pl.program_id(1) == 0)
    def _():
        xpad_ref[0:SUBLANES, :] = jnp.zeros((SUBLANES, SSD_CONV_DIM), F32)
        st_ref[...] = jnp.zeros_like(st_ref)

    xpad_ref[SUBLANES:SUBLANES + rows, :] = xbc_ref[...]
    acc = cb_ref[...] + xpad_ref[SUBLANES:SUBLANES + rows, :] * cw_ref[SSD_CONV - 1:SSD_CONV, :]
    for j in range(SSD_CONV - 1):
        off = SUBLANES - (SSD_CONV - 1) + j
        acc = acc + xpad_ref[off:off + rows, :] * cw_ref[j:j + 1, :]
    xpad_ref[0:SUBLANES, :] = xpad_ref[rows:rows + SUBLANES, :]
    xbc = jax.nn.silu(acc)

    lane = lax.broadcasted_iota(jnp.int32, (1, SMALL_WIDTH), 1)
    is_dt = (lane >= DT_LANE0) & (lane < DT_LANE0 + SSD_HEADS)
    dt = jnp.where(is_dt, jax.nn.softplus(s_ref[...] + dtb_ref[...]), 0.0)
    da = dt * jnp.where(is_dt, -jnp.exp(alog_ref[...]), 0.0)

    chunks = range(rows // CHUNK)
    groups = range(SSD_GROUPS)
    rsl = lambda c: slice(c * CHUNK, (c + 1) * CHUNK)
    gsl = lambda g: slice(g * SSD_GROUP_WIDTH, (g + 1) * SSD_GROUP_WIDTH)
    e2 = e2_ref[...]

    def expand(x):
        hi, lo = _split2(x)
        return _dot(jnp.concatenate([hi, lo], axis=1), e2)

    cum_lanes = _chunk_cumsum(jnp.concatenate([da[rsl(c)] for c in chunks], axis=1), _tri2())
    cum = [cum_lanes[:, c * SMALL_WIDTH:(c + 1) * SMALL_WIDTH] for c in chunks]

    cum_e = expand(jnp.concatenate(cum, axis=0))
    dt_e = expand(dt)

    pair_rows = []
    for c in chunks:
        cum_t = cum[c].T
        pair_rows.append(jnp.concatenate(
            [cum_t[DT_LANE0:DT_LANE0 + SSD_HEADS], cum_t[DT_LANE0 + 1:DT_LANE0 + SSD_HEADS + 1]], axis=1))

    xs = xbc[:, :SSD_WIDTH]
    xdt = xs * dt_e
    xdt_b = xdt.astype(BF16)
    decay_in = jnp.exp(cum_e)
    cum_last = [cum_e[(c + 1) * CHUNK - 1:(c + 1) * CHUNK, :] for c in chunks]
    decay_last = [jnp.exp(cl) for cl in cum_last]
    xd_end = [(xdt[rsl(c)] * jnp.exp(cum_last[c] - cum_e[rsl(c)])).astype(BF16) for c in chunks]
    b_bf = xbc[:, SSD_WIDTH:SSD_WIDTH + SSD_BC_WIDTH].astype(BF16)
    c_bf = xbc[:, SSD_WIDTH + SSD_BC_WIDTH:].astype(BF16)
    nsl = lambda g: slice(g * SSD_STATE, (g + 1) * SSD_STATE)

    quad = 4 * SSD_HEAD_DIM
    r4 = lax.broadcasted_iota(jnp.int32, (CHUNK, quad), 0)
    l4 = lax.broadcasted_iota(jnp.int32, (CHUNK, quad), 1)
    causal4 = (l4 & (SSD_HEAD_DIM - 1)) <= r4
    rb = lax.broadcasted_iota(jnp.int32, (quad, quad), 0)
    lb = lax.broadcasted_iota(jnp.int32, (quad, quad), 1)
    block_diag = (rb // SSD_HEAD_DIM) == (lb // SSD_HEAD_DIM)
    cb4 = [[_dot_nt(c_bf[rsl(c), nsl(g)], jnp.concatenate([b_bf[rsl(c), nsl(g)]] * 4, axis=0))
            for g in groups] for c in chunks]
    upd = [[_dot_tn(b_bf[rsl(c), nsl(g)], xd_end[c][:, gsl(g)]) for g in groups] for c in chunks]
    intra = []
    for c in chunks:
        parts = []
        for g in groups:
            for qd in range(SSD_HEADS_PER_GROUP // 4):
                h0 = g * SSD_HEADS_PER_GROUP + qd * 4
                ls = slice(h0 * SSD_HEAD_DIM, h0 * SSD_HEAD_DIM + quad)
                row = jnp.concatenate([pair_rows[c][h0:h0 + 1], pair_rows[c][h0 + 2:h0 + 3]], axis=1)
                seg = cum_e[rsl(c), ls] - row
                l_q = jnp.where(causal4, cb4[c][g] * jnp.exp(jnp.minimum(seg, 0.0)), 0.0).astype(BF16)
                x_q = xdt_b[rsl(c), ls]
                x_bd = jnp.where(block_diag, jnp.concatenate([x_q] * 4, axis=0), jnp.zeros((), BF16))
                parts.append(_dot(l_q, x_bd))
        intra.append(jnp.concatenate(parts, axis=1))

    state = [st_ref[g] for g in groups]
    states = []
    for c in chunks:
        states.append([s.astype(BF16) for s in state])
        state = [state[g] * decay_last[c][:, gsl(g)] + upd[c][g] for g in groups]
    for g in groups:
        st_ref[g] = state[g]

    dskip = dskip_ref[...]
    nw = nw_ref[...]
    for c in chunks:
        inter = jnp.concatenate([_dot(c_bf[rsl(c), nsl(g)], states[c][g]) for g in groups], axis=1)
        y = intra[c] + inter * decay_in[rsl(c)] + xs[rsl(c)] * dskip
        y = y * jax.nn.silu(z_ref[rsl(c), :])
        outs = [_rmsnorm(y[:, gsl(g)], nw[:, gsl(g)]) for g in groups]
        o_ref[rsl(c), :] = jnp.concatenate(outs, axis=1).astype(o_ref.dtype)


def _ssd(z, xbc, small, conv_w, conv_b, dtb_pad, alog_pad, e2, dskip_e, norm_w):
    bsz, t, _ = z.shape
    tb = SSD_ROWS
    grid = (bsz, t // tb)
    blk = lambda w: pl.BlockSpec((None, tb, w), lambda b, i: (b, i, 0))
    fixed = lambda shape: pl.BlockSpec(shape, lambda b, i: (0,) * len(shape))
    in_specs = [
        blk(SSD_WIDTH), blk(SSD_CONV_DIM), blk(SMALL_WIDTH),
        fixed(conv_w.shape), fixed(conv_b.shape), fixed(dtb_pad.shape), fixed(alog_pad.shape),
        fixed(e2.shape), fixed(dskip_e.shape), fixed(norm_w.shape),
    ]
    return pl.pallas_call(
        _ssd_kernel,
        grid=grid,
        in_specs=in_specs,
        out_specs=blk(SSD_WIDTH),
        out_shape=jax.ShapeDtypeStruct((bsz, t, SSD_WIDTH), BF16),
        scratch_shapes=[
            pltpu.VMEM((tb + SUBLANES, SSD_CONV_DIM), F32),
            pltpu.VMEM((SSD_GROUPS, SSD_STATE, SSD_GROUP_WIDTH), F32),
        ],
        compiler_params=pltpu.CompilerParams(
            dimension_semantics=("arbitrary", "arbitrary"), vmem_limit_bytes=VMEM_LIMIT),
        name="ssd",
    )(z, xbc, small, conv_w, conv_b, dtb_pad, alog_pad, e2, dskip_e, norm_w)


def _out_proj_kernel(x_ref, gla_ref, ssd_ref, p_ref, wo1_ref, wo2_ref, wpe_ref, wgate_ref,
                     pnw_ref, fnw_ref, o_ref, *, final):
    h = x_ref[...] + _dot(gla_ref[...], wo1_ref[...]) + _dot(ssd_ref[...], wo2_ref[...])
    hn = _rmsnorm(h, pnw_ref[...]).astype(BF16)
    gate = jax.nn.sigmoid(_dot(hn, wgate_ref[...]))
    h = h + gate * _dot(p_ref[...].astype(BF16), wpe_ref[...])
    if final:
        h = _rmsnorm(h, fnw_ref[...])
    o_ref[...] = h


def _out_proj(x2, gla, ssd, p2, wo1, wo2, wpe, wgate, pe_norm_w, final_norm_w, final):
    n, d = x2.shape
    tm = OUT_PROJ_ROWS
    row = lambda w: pl.BlockSpec((tm, w), lambda i: (i, 0))
    fixed = lambda a: pl.BlockSpec(a.shape, lambda i: (0, 0))
    in_specs = [row(d), row(gla.shape[1]), row(ssd.shape[1]), row(p2.shape[1]),
                fixed(wo1), fixed(wo2), fixed(wpe), fixed(wgate), fixed(pe_norm_w), fixed(final_norm_w)]
    return pl.pallas_call(
        functools.partial(_out_proj_kernel, final=final),
        grid=(n // tm,),
        in_specs=in_specs,
        out_specs=row(d),
        out_shape=jax.ShapeDtypeStruct((n, d), F32),
        compiler_params=pltpu.CompilerParams(
            dimension_semantics=("arbitrary",), vmem_limit_bytes=VMEM_LIMIT),
        name="out_proj",
    )(x2, gla, ssd, p2, wo1, wo2, wpe, wgate, pe_norm_w, final_norm_w)


def _expansion_matrix():
    e = np.zeros((2 * SMALL_WIDTH, SSD_WIDTH), np.float32)
    for h in range(SSD_HEADS):
        e[DT_LANE0 + h, h * SSD_HEAD_DIM:(h + 1) * SSD_HEAD_DIM] = 1.0
        e[SMALL_WIDTH + DT_LANE0 + h, h * SSD_HEAD_DIM:(h + 1) * SSD_HEAD_DIM] = 1.0
    return jnp.asarray(e, BF16)


def _pad_small(v):
    return jnp.zeros((1, SMALL_WIDTH), F32).at[0, DT_LANE0:DT_LANE0 + SSD_HEADS].set(v.astype(F32))


def _layer(h2, p2, bsz, t, final, norm_w, w_in, gla_gate_up, gla_gate_b, gla_norm_w, conv_w, conv_b,
           dt_bias, a_log, d_skip, ssd_norm_w, w_out, w_pe, w_pe_gate, pe_norm_w, final_norm_w):
    d = h2.shape[1]
    sizes = (GLA_KEY_WIDTH, GLA_KEY_WIDTH, GLA_WIDTH, GLA_WIDTH, GLA_GATE_RANK, SSD_WIDTH,
             SSD_CONV_DIM, SSD_HEADS)
    offs = np.concatenate([[0], np.cumsum(sizes)])
    wq, wk, wv, wg, wlr, wz, wx, wdt = [w_in[:, offs[i]:offs[i + 1]] for i in range(len(sizes))]
    pad = jnp.zeros((d, SMALL_WIDTH - GLA_GATE_RANK - SSD_HEADS), w_in.dtype)
    wsmall = jnp.concatenate([wlr, wdt, pad], axis=1)
    weights = [w.astype(BF16) for w in (wq, wk, wv, wg, wz, wx, wsmall)]

    q, k, v, g, z, xbc, small = _in_proj(h2, norm_w.reshape(1, d).astype(F32), weights)
    r3 = lambda a: a.reshape(bsz, t, a.shape[1])

    up_pad = jnp.zeros((SMALL_WIDTH, GLA_KEY_WIDTH), BF16).at[:GLA_GATE_RANK].set(gla_gate_up.astype(BF16))
    gla = _gla(r3(q), r3(k), r3(v), r3(g), r3(small), up_pad,
               gla_gate_b.reshape(1, -1).astype(F32), gla_norm_w.reshape(1, -1).astype(F32))

    ssd = _ssd(r3(z), r3(xbc), r3(small), conv_w.astype(F32), conv_b.reshape(1, -1).astype(F32),
               _pad_small(dt_bias), _pad_small(a_log), _expansion_matrix(),
               jnp.repeat(d_skip.astype(F32), SSD_HEAD_DIM).reshape(1, -1),
               ssd_norm_w.reshape(1, -1).astype(F32))

    n = bsz * t
    return _out_proj(h2, gla.reshape(n, GLA_WIDTH), ssd.reshape(n, SSD_WIDTH), p2,
                     w_out[:GLA_WIDTH].astype(BF16), w_out[GLA_WIDTH:].astype(BF16),
                     w_pe.astype(BF16), w_pe_gate.astype(BF16),
                     pe_norm_w.reshape(1, d).astype(F32), final_norm_w.reshape(1, d).astype(F32), final)


def kernel(x, p, norm_w, w_in, gla_gate_up, gla_gate_b, gla_norm_w, conv_w, conv_b, dt_bias, a_log,
           d_skip, ssd_norm_w, w_out, w_pe, w_pe_gate, pe_norm_w, final_norm_w):
    bsz, t, d = x.shape
    depth = p.shape[0]
    h2 = x.reshape(bsz * t, d)
    for i in range(depth):
        h2 = _layer(h2, p[i].reshape(bsz * t, -1), bsz, t, i == depth - 1,
                    norm_w[i], w_in[i], gla_gate_up[i], gla_gate_b[i], gla_norm_w[i], conv_w[i],
                    conv_b[i], dt_bias[i], a_log[i], d_skip[i], ssd_norm_w[i], w_out[i], w_pe[i],
                    w_pe_gate[i], pe_norm_w[i], final_norm_w)
    return h2.reshape(bsz, t, d)
```

```python
import functools

import jax
import jax.numpy as jnp
import numpy as np
from jax import lax
from jax.experimental import pallas as pl
from jax.experimental.pallas import tpu as pltpu

F32 = jnp.float32
BF16 = jnp.bfloat16

EPS = 1e-6
LOG2_E = 1.4426950408889634
CHUNK = 64
GLA_HEADS = 4
GLA_HEAD_K = 128
GLA_HEAD_V = 256
GLA_KEY_WIDTH = GLA_HEADS * GLA_HEAD_K
GLA_WIDTH = GLA_HEADS * GLA_HEAD_V
GLA_GATE_RANK = 16
GLA_GATE_NORMALIZER = 16.0
SSD_HEADS = 16
SSD_HEAD_DIM = 64
SSD_WIDTH = SSD_HEADS * SSD_HEAD_DIM
SSD_GROUPS = 2
SSD_HEADS_PER_GROUP = SSD_HEADS // SSD_GROUPS
SSD_GROUP_WIDTH = SSD_WIDTH // SSD_GROUPS
SSD_STATE = 128
SSD_CONV = 4
SSD_BC_WIDTH = SSD_GROUPS * SSD_STATE
SSD_CONV_DIM = SSD_WIDTH + 2 * SSD_BC_WIDTH
LANES = 128
SUBLANES = 8
SMALL_WIDTH = LANES
DT_LANE0 = GLA_GATE_RANK

IN_PROJ_ROWS = 256
GLA_ROWS = 512
SSD_ROWS = 256
OUT_PROJ_ROWS = 512
assert IN_PROJ_ROWS // CHUNK <= SUBLANES and GLA_ROWS % IN_PROJ_ROWS == 0
VMEM_LIMIT = 56 * 1024 * 1024


def _rmsnorm(x, w):
    return x * lax.rsqrt(jnp.mean(x * x, axis=-1, keepdims=True) + EPS) * w


def _dot(a, b):
    return jnp.dot(a, b, preferred_element_type=F32)


def _dot_nt(a, b):
    return lax.dot_general(a, b, (((1,), (1,)), ((), ())), preferred_element_type=F32)


def _dot_tn(a, b):
    return lax.dot_general(a, b, (((0,), (0,)), ((), ())), preferred_element_type=F32)


def _split2(x):
    hi = x.astype(BF16)
    lo = (x - hi.astype(F32)).astype(BF16)
    return hi, lo


def _chunk_cumsum(x, tri2):
    hi, lo = _split2(x)
    return _dot(tri2, jnp.concatenate([hi, lo], axis=0))


def _tri2():
    r = lax.broadcasted_iota(jnp.int32, (CHUNK, 2 * CHUNK), 0)
    c = lax.broadcasted_iota(jnp.int32, (CHUNK, 2 * CHUNK), 1)
    c = jnp.where(c >= CHUNK, c - CHUNK, c)
    return jnp.where(r >= c, 1.0, 0.0).astype(BF16)


def _causal():
    r = lax.broadcasted_iota(jnp.int32, (CHUNK, CHUNK), 0)
    c = lax.broadcasted_iota(jnp.int32, (CHUNK, CHUNK), 1)
    return r >= c


def _in_proj_kernel(x_ref, nw_ref, ws_ref, up_ref, gb_ref, wx_ref, cw_ref, cb_ref,
                    wq_ref, wk_ref, wv_ref, wg_ref, wz_ref,
                    s_ref, xs_ref, bc_ref, qd_ref, ki_ref, kd_ref, dl_ref, v_ref, sg_ref, sz_ref,
                    tail_ref, *, blocks_per_seq):
    rows = x_ref.shape[0]
    chunks = range(rows // CHUNK)
    rsl = lambda c: slice(c * CHUNK, (c + 1) * CHUNK)

    @pl.when(pl.program_id(0) % blocks_per_seq == 0)
    def _():
        tail_ref[...] = jnp.zeros_like(tail_ref)

    u = _rmsnorm(x_ref[...], nw_ref[...]).astype(BF16)

    small = _dot(u, ws_ref[...])
    s_ref[...] = small
    gate = _dot(small.astype(BF16), up_ref[...]) + gb_ref[...]
    log2_a = jax.nn.log_sigmoid(gate) * (LOG2_E / GLA_GATE_NORMALIZER)

    xr = _dot(u, wx_ref[...])
    q = _dot(u, wq_ref[...]) * (GLA_HEAD_K ** -0.5)
    k = _dot(u, wk_ref[...])
    v_ref[...] = _dot(u, wv_ref[...])

    row8 = lax.broadcasted_iota(jnp.int32, (SUBLANES, 1), 0)

    def shift_rows(a, prev_tile, s):
        rolled = pltpu.roll(a, s, 0)
        head = jnp.where(row8 < s, pltpu.roll(prev_tile, s, 0), rolled[:SUBLANES])
        return jnp.concatenate([head, rolled[SUBLANES:]], axis=0)

    x1 = shift_rows(xr, tail_ref[0], 1)
    pair_lo = xr * cw_ref[1:2, :] + x1 * cw_ref[0:1, :]
    acc = cb_ref[...] + (xr * cw_ref[3:4, :] + x1 * cw_ref[2:3, :]) + shift_rows(pair_lo, tail_ref[1], 2)
    tail_ref[0] = xr[rows - SUBLANES:]
    tail_ref[1] = pair_lo[rows - SUBLANES:]
    xbc = jax.nn.silu(acc)
    xs_ref[...] = xbc[:, :SSD_WIDTH]
    bc_ref[...] = xbc[:, SSD_WIDTH:].astype(BF16)

    b_all = _chunk_cumsum(jnp.concatenate([log2_a[rsl(c)] for c in chunks], axis=1), _tri2())
    dl_ref[...] = jnp.zeros_like(dl_ref)
    for c in chunks:
        b = b_all[:, c * GLA_KEY_WIDTH:(c + 1) * GLA_KEY_WIDTH]
        b_last = b[CHUNK - 1:CHUNK, :]
        qd_ref[rsl(c), :] = (q[rsl(c)] * jnp.exp2(b)).astype(BF16)
        ki_ref[rsl(c), :] = (k[rsl(c)] * jnp.exp2(-b)).astype(BF16)
        kd_ref[rsl(c), :] = (k[rsl(c)] * jnp.exp2(b_last - b)).astype(BF16)
        dl_ref[c:c + 1, :] = jnp.exp2(b_last)

    sg_ref[...] = jax.nn.silu(_dot(u, wg_ref[...]))
    sz_ref[...] = jax.nn.silu(_dot(u, wz_ref[...]))


def _in_proj(x2, blocks_per_seq, norm_w, ws, up_pad, gate_b, wx, conv_w, conv_b, wq, wk, wv, wg, wz):
    n, d = x2.shape
    tm = IN_PROJ_ROWS
    row = lambda w: pl.BlockSpec((tm, w), lambda i: (i, 0))
    fixed = lambda a: pl.BlockSpec(a.shape, lambda i: (0, 0))
    consts = (norm_w, ws, up_pad, gate_b, wx, conv_w, conv_b, wq, wk, wv, wg, wz)
    outs = [
        (SMALL_WIDTH, F32), (SSD_WIDTH, F32), (2 * SSD_BC_WIDTH, BF16),
        (GLA_KEY_WIDTH, BF16), (GLA_KEY_WIDTH, BF16), (GLA_KEY_WIDTH, BF16), None,
        (GLA_WIDTH, F32), (GLA_WIDTH, F32), (SSD_WIDTH, F32),
    ]
    out_specs, out_shape = [], []
    for o in outs:
        if o is None:
            out_specs.append(pl.BlockSpec((SUBLANES, GLA_KEY_WIDTH), lambda i: (i, 0)))
            out_shape.append(jax.ShapeDtypeStruct((n // tm * SUBLANES, GLA_KEY_WIDTH), F32))
        else:
            out_specs.append(row(o[0]))
            out_shape.append(jax.ShapeDtypeStruct((n, o[0]), o[1]))
    return pl.pallas_call(
        functools.partial(_in_proj_kernel, blocks_per_seq=blocks_per_seq),
        grid=(n // tm,),
        in_specs=[row(d)] + [fixed(a) for a in consts],
        out_specs=out_specs,
        out_shape=out_shape,
        scratch_shapes=[pltpu.VMEM((2, SUBLANES, SSD_CONV_DIM), F32)],
        compiler_params=pltpu.CompilerParams(
            dimension_semantics=("arbitrary",), vmem_limit_bytes=VMEM_LIMIT),
        name="in_proj",
    )(x2, *consts)


def _gla_kernel(qd_ref, ki_ref, kd_ref, v_ref, sg_ref, dl_ref, nw_ref, o_ref, st_ref):
    @pl.when(pl.program_id(2) == 0)
    def _():
        st_ref[...] = jnp.zeros_like(st_ref)

    rows = qd_ref.shape[0]
    chunks = range(rows // CHUNK)
    rsl = lambda c: pl.ds(c * CHUNK, CHUNK)
    causal = _causal()
    per_block = IN_PROJ_ROWS // CHUNK

    scores = [jnp.where(causal, _dot_nt(qd_ref[rsl(c), :], ki_ref[rsl(c), :]), 0.0).astype(BF16)
              for c in chunks]
    v = [v_ref[rsl(c), :].astype(BF16) for c in chunks]
    upd = [_dot_tn(v[c], kd_ref[rsl(c), :]) for c in chunks]

    state = st_ref[...]
    states = []
    for c in chunks:
        r = (c // per_block) * SUBLANES + c % per_block
        states.append(state.astype(BF16))
        state = state * dl_ref[r:r + 1, :] + upd[c]
    st_ref[...] = state

    nw = nw_ref[...]
    for c in chunks:
        o = _dot(scores[c], v[c]) + _dot_nt(qd_ref[rsl(c), :], states[c])
        o_ref[rsl(c), :] = (_rmsnorm(o, nw) * sg_ref[rsl(c), :]).astype(o_ref.dtype)


def _gla(qd, ki, kd, v, sg, dl, norm_w):
    bsz, t, _ = qd.shape
    tb = GLA_ROWS
    grid = (bsz, GLA_HEADS, t // tb)
    head = lambda w: pl.BlockSpec((None, tb, w), lambda b, h, i: (b, i, h))
    in_specs = [
        head(GLA_HEAD_K), head(GLA_HEAD_K), head(GLA_HEAD_K), head(GLA_HEAD_V), head(GLA_HEAD_V),
        pl.BlockSpec((None, tb // IN_PROJ_ROWS * SUBLANES, GLA_HEAD_K), lambda b, h, i: (b, i, h)),
        pl.BlockSpec((1, GLA_HEAD_V), lambda b, h, i: (0, 0)),
    ]
    return pl.pallas_call(
        _gla_kernel,
        grid=grid,
        in_specs=in_specs,
        out_specs=head(GLA_HEAD_V),
        out_shape=jax.ShapeDtypeStruct((bsz, t, GLA_WIDTH), BF16),
        scratch_shapes=[pltpu.VMEM((GLA_HEAD_V, GLA_HEAD_K), F32)],
        compiler_params=pltpu.CompilerParams(
            dimension_semantics=("arbitrary", "arbitrary", "arbitrary"),
            vmem_limit_bytes=VMEM_LIMIT),
        name="gla",
    )(qd, ki, kd, v, sg, dl, norm_w)


def _ssd_kernel(sz_ref, xs_ref, bc_ref, s_ref, dtb_ref, alog_ref, e2_ref, dskip_ref,
                nw_ref, o_ref, st_ref):
    rows = xs_ref.shape[0]

    @pl.when(pl.program_id(1) == 0)
    def _():
        st_ref[...] = jnp.zeros_like(st_ref)

    lane = lax.broadcasted_iota(jnp.int32, (1, SMALL_WIDTH), 1)
    is_dt = (lane >= DT_LANE0) & (lane < DT_LANE0 + SSD_HEADS)
    dt = jnp.where(is_dt, jax.nn.softplus(s_ref[...] + dtb_ref[...]), 0.0)
    da = dt * jnp.where(is_dt, -jnp.exp(alog_ref[...]), 0.0)

    chunks = range(rows // CHUNK)
    groups = range(SSD_GROUPS)
    rsl = lambda c: slice(c * CHUNK, (c + 1) * CHUNK)
    gsl = lambda g: slice(g * SSD_GROUP_WIDTH, (g + 1) * SSD_GROUP_WIDTH)
    e2 = e2_ref[...]

    def expand(x):
        hi, lo = _split2(x)
        return _dot(jnp.concatenate([hi, lo], axis=1), e2)

    cum_lanes = _chunk_cumsum(jnp.concatenate([da[rsl(c)] for c in chunks], axis=1), _tri2())
    cum = [cum_lanes[:, c * SMALL_WIDTH:(c + 1) * SMALL_WIDTH] for c in chunks]

    cum_e = expand(jnp.concatenate(cum, axis=0))
    dt_e = expand(dt)

    pair_rows = []
    for c in chunks:
        cum_t = cum[c].T
        pair_rows.append(jnp.concatenate(
            [cum_t[DT_LANE0:DT_LANE0 + SSD_HEADS], cum_t[DT_LANE0 + 1:DT_LANE0 + SSD_HEADS + 1]], axis=1))

    xs = xs_ref[...]
    xdt = xs * dt_e
    xdt_b = xdt.astype(BF16)
    decay_in = jnp.exp(cum_e)
    cum_last = [cum_e[(c + 1) * CHUNK - 1:(c + 1) * CHUNK, :] for c in chunks]
    decay_last = [jnp.exp(cl) for cl in cum_last]
    xd_end = [(xdt[rsl(c)] * jnp.exp(cum_last[c] - cum_e[rsl(c)])).astype(BF16) for c in chunks]
    bsl = lambda g: slice(g * SSD_STATE, (g + 1) * SSD_STATE)
    csl = lambda g: slice(SSD_BC_WIDTH + g * SSD_STATE, SSD_BC_WIDTH + (g + 1) * SSD_STATE)

    quad = 4 * SSD_HEAD_DIM
    r4 = lax.broadcasted_iota(jnp.int32, (CHUNK, quad), 0)
    l4 = lax.broadcasted_iota(jnp.int32, (CHUNK, quad), 1)
    causal4 = (l4 & (SSD_HEAD_DIM - 1)) <= r4
    rb = lax.broadcasted_iota(jnp.int32, (quad, quad), 0)
    lb = lax.broadcasted_iota(jnp.int32, (quad, quad), 1)
    block_diag = (rb // SSD_HEAD_DIM) == (lb // SSD_HEAD_DIM)
    cb4 = [[_dot_nt(bc_ref[rsl(c), csl(g)], jnp.concatenate([bc_ref[rsl(c), bsl(g)]] * 4, axis=0))
            for g in groups] for c in chunks]
    upd = [[_dot_tn(bc_ref[rsl(c), bsl(g)], xd_end[c][:, gsl(g)]) for g in groups] for c in chunks]
    intra = []
    for c in chunks:
        parts = []
        for g in groups:
            for qd in range(SSD_HEADS_PER_GROUP // 4):
                h0 = g * SSD_HEADS_PER_GROUP + qd * 4
                ls = slice(h0 * SSD_HEAD_DIM, h0 * SSD_HEAD_DIM + quad)
                row = jnp.concatenate([pair_rows[c][h0:h0 + 1], pair_rows[c][h0 + 2:h0 + 3]], axis=1)
                seg = cum_e[rsl(c), ls] - row
                l_q = jnp.where(causal4, cb4[c][g] * jnp.exp(jnp.minimum(seg, 0.0)), 0.0).astype(BF16)
                x_q = xdt_b[rsl(c), ls]
                x_bd = jnp.where(block_diag, jnp.concatenate([x_q] * 4, axis=0), jnp.zeros((), BF16))
                parts.append(_dot(l_q, x_bd))
        intra.append(jnp.concatenate(parts, axis=1))

    state = [st_ref[g] for g in groups]
    states = []
    for c in chunks:
        states.append([s.astype(BF16) for s in state])
        state = [state[g] * decay_last[c][:, gsl(g)] + upd[c][g] for g in groups]
    for g in groups:
        st_ref[g] = state[g]

    dskip = dskip_ref[...]
    nw = nw_ref[...]
    for c in chunks:
        inter = jnp.concatenate([_dot(bc_ref[rsl(c), csl(g)], states[c][g]) for g in groups], axis=1)
        y = intra[c] + inter * decay_in[rsl(c)] + xs[rsl(c)] * dskip
        y = y * sz_ref[rsl(c), :]
        outs = [_rmsnorm(y[:, gsl(g)], nw[:, gsl(g)]) for g in groups]
        o_ref[rsl(c), :] = jnp.concatenate(outs, axis=1).astype(o_ref.dtype)


def _ssd(sz, xs, bc, small, dtb_pad, alog_pad, e2, dskip_e, norm_w):
    bsz, t, _ = sz.shape
    tb = SSD_ROWS
    grid = (bsz, t // tb)
    blk = lambda w: pl.BlockSpec((None, tb, w), lambda b, i: (b, i, 0))
    fixed = lambda a: pl.BlockSpec(a.shape, lambda b, i: (0,) * a.ndim)
    consts = (dtb_pad, alog_pad, e2, dskip_e, norm_w)
    in_specs = [blk(SSD_WIDTH), blk(SSD_WIDTH), blk(2 * SSD_BC_WIDTH), blk(SMALL_WIDTH)]
    in_specs += [fixed(a) for a in consts]
    return pl.pallas_call(
        _ssd_kernel,
        grid=grid,
        in_specs=in_specs,
        out_specs=blk(SSD_WIDTH),
        out_shape=jax.ShapeDtypeStruct((bsz, t, SSD_WIDTH), BF16),
        scratch_shapes=[pltpu.VMEM((SSD_GROUPS, SSD_STATE, SSD_GROUP_WIDTH), F32)],
        compiler_params=pltpu.CompilerParams(
            dimension_semantics=("arbitrary", "arbitrary"), vmem_limit_bytes=VMEM_LIMIT),
        name="ssd",
    )(sz, xs, bc, small, *consts)


def _out_proj_kernel(x_ref, gla_ref, ssd_ref, p_ref, wo1_ref, wo2_ref, wpe_ref, wgate_ref,
                     pnw_ref, fnw_ref, o_ref, *, final):
    h = x_ref[...] + _dot(gla_ref[...], wo1_ref[...]) + _dot(ssd_ref[...], wo2_ref[...])
    hn = _rmsnorm(h, pnw_ref[...]).astype(BF16)
    gate = jax.nn.sigmoid(_dot(hn, wgate_ref[...]))
    h = h + gate * _dot(p_ref[...].astype(BF16), wpe_ref[...])
    if final:
        h = _rmsnorm(h, fnw_ref[...])
    o_ref[...] = h


def _out_proj(x2, gla, ssd, p2, wo1, wo2, wpe, wgate, pe_norm_w, final_norm_w, final):
    n, d = x2.shape
    tm = OUT_PROJ_ROWS
    row = lambda w: pl.BlockSpec((tm, w), lambda i: (i, 0))
    fixed = lambda a: pl.BlockSpec(a.shape, lambda i: (0, 0))
    in_specs = [row(d), row(gla.shape[1]), row(ssd.shape[1]), row(p2.shape[1]),
                fixed(wo1), fixed(wo2), fixed(wpe), fixed(wgate), fixed(pe_norm_w), fixed(final_norm_w)]
    return pl.pallas_call(
        functools.partial(_out_proj_kernel, final=final),
        grid=(n // tm,),
        in_specs=in_specs,
        out_specs=row(d),
        out_shape=jax.ShapeDtypeStruct((n, d), F32),
        compiler_params=pltpu.CompilerParams(
            dimension_semantics=("arbitrary",), vmem_limit_bytes=VMEM_LIMIT),
        name="out_proj",
    )(x2, gla, ssd, p2, wo1, wo2, wpe, wgate, pe_norm_w, final_norm_w)


def _expansion_matrix():
    e = np.zeros((2 * SMALL_WIDTH, SSD_WIDTH), np.float32)
    for h in range(SSD_HEADS):
        e[DT_LANE0 + h, h * SSD_HEAD_DIM:(h + 1) * SSD_HEAD_DIM] = 1.0
        e[SMALL_WIDTH + DT_LANE0 + h, h * SSD_HEAD_DIM:(h + 1) * SSD_HEAD_DIM] = 1.0
    return jnp.asarray(e, BF16)


def _pad_small(v):
    return jnp.zeros((1, SMALL_WIDTH), F32).at[0, DT_LANE0:DT_LANE0 + SSD_HEADS].set(v.astype(F32))


def _layer(h2, p2, bsz, t, final, norm_w, w_in, gla_gate_up, gla_gate_b, gla_norm_w, conv_w, conv_b,
           dt_bias, a_log, d_skip, ssd_norm_w, w_out, w_pe, w_pe_gate, pe_norm_w, final_norm_w):
    d = h2.shape[1]
    sizes = (GLA_KEY_WIDTH, GLA_KEY_WIDTH, GLA_WIDTH, GLA_WIDTH, GLA_GATE_RANK, SSD_WIDTH,
             SSD_CONV_DIM, SSD_HEADS)
    offs = np.concatenate([[0], np.cumsum(sizes)])
    wq, wk, wv, wg, wlr, wz, wx, wdt = [w_in[:, offs[i]:offs[i + 1]] for i in range(len(sizes))]
    pad = jnp.zeros((d, SMALL_WIDTH - GLA_GATE_RANK - SSD_HEADS), w_in.dtype)
    wsmall = jnp.concatenate([wlr, wdt, pad], axis=1)
    wq, wk, wv, wg, wz, wx, wsmall = [w.astype(BF16) for w in (wq, wk, wv, wg, wz, wx, wsmall)]
    up_pad = jnp.zeros((SMALL_WIDTH, GLA_KEY_WIDTH), BF16).at[:GLA_GATE_RANK].set(gla_gate_up.astype(BF16))

    small, xs, bc, qd, ki, kd, dl, v, sg, sz = _in_proj(
        h2, t // IN_PROJ_ROWS, norm_w.reshape(1, d).astype(F32), wsmall, up_pad,
        gla_gate_b.reshape(1, -1).astype(F32), wx, conv_w.astype(F32), conv_b.reshape(1, -1).astype(F32),
        wq, wk, wv, wg, wz)
    r3 = lambda a: a.reshape(bsz, a.shape[0] // bsz, a.shape[1])

    gla = _gla(r3(qd), r3(ki), r3(kd), r3(v), r3(sg), r3(dl), gla_norm_w.reshape(1, -1).astype(F32))

    ssd = _ssd(r3(sz), r3(xs), r3(bc), r3(small), _pad_small(dt_bias), _pad_small(a_log),
               _expansion_matrix(), jnp.repeat(d_skip.astype(F32), SSD_HEAD_DIM).reshape(1, -1),
               ssd_norm_w.reshape(1, -1).astype(F32))

    n = bsz * t
    return _out_proj(h2, gla.reshape(n, GLA_WIDTH), ssd.reshape(n, SSD_WIDTH), p2,
                     w_out[:GLA_WIDTH].astype(BF16), w_out[GLA_WIDTH:].astype(BF16),
                     w_pe.astype(BF16), w_pe_gate.astype(BF16),
                     pe_norm_w.reshape(1, d).astype(F32), final_norm_w.reshape(1, d).astype(F32), final)


def kernel(x, p, norm_w, w_in, gla_gate_up, gla_gate_b, gla_norm_w, conv_w, conv_b, dt_bias, a_log,
           d_skip, ssd_norm_w, w_out, w_pe, w_pe_gate, pe_norm_w, final_norm_w):
    bsz, t, d = x.shape
    depth = p.shape[0]
    h2 = x.reshape(bsz * t, d)
    for i in range(depth):
        h2 = _layer(h2, p[i].reshape(bsz * t, -1), bsz, t, i == depth - 1,
                    norm_w[i], w_in[i], gla_gate_up[i], gla_gate_b[i], gla_norm_w[i], conv_w[i],
                    conv_b[i], dt_bias[i], a_log[i], d_skip[i], ssd_norm_w[i], w_out[i], w_pe[i],
                    w_pe_gate[i], pe_norm_w[i], final_norm_w)
    return h2.reshape(bsz, t, d)
```

```python
import functools

import jax
import jax.numpy as jnp
import numpy as np
from jax import lax
from jax.experimental import pallas as pl
from jax.experimental.pallas import tpu as pltpu

F32 = jnp.float32
BF16 = jnp.bfloat16

EPS = 1e-6
LOG2_E = 1.4426950408889634
CHUNK = 64
GLA_HEADS = 4
GLA_HEAD_K = 128
GLA_HEAD_V = 256
GLA_KEY_WIDTH = GLA_HEADS * GLA_HEAD_K
GLA_WIDTH = GLA_HEADS * GLA_HEAD_V
GLA_GATE_RANK = 16
GLA_GATE_NORMALIZER = 16.0
SSD_HEADS = 16
SSD_HEAD_DIM = 64
SSD_WIDTH = SSD_HEADS * SSD_HEAD_DIM
SSD_GROUPS = 2
SSD_HEADS_PER_GROUP = SSD_HEADS // SSD_GROUPS
SSD_GROUP_WIDTH = SSD_WIDTH // SSD_GROUPS
SSD_STATE = 128
SSD_CONV = 4
SSD_BC_WIDTH = SSD_GROUPS * SSD_STATE
SSD_CONV_DIM = SSD_WIDTH + 2 * SSD_BC_WIDTH
D_MIX = GLA_WIDTH + SSD_WIDTH
LANES = 128
SUBLANES = 8
SMALL_WIDTH = LANES
DT_LANE0 = GLA_GATE_RANK

BLOCK_ROWS = 256
assert BLOCK_ROWS // CHUNK <= SUBLANES
VMEM_LIMIT = 58 * 1024 * 1024


def _rmsnorm(x, w):
    return x * lax.rsqrt(jnp.mean(x * x, axis=-1, keepdims=True) + EPS) * w


def _dot(a, b):
    return jnp.dot(a, b, preferred_element_type=F32)


def _dot_nt(a, b):
    return lax.dot_general(a, b, (((1,), (1,)), ((), ())), preferred_element_type=F32)


def _dot_tn(a, b):
    return lax.dot_general(a, b, (((0,), (0,)), ((), ())), preferred_element_type=F32)


def _split2(x):
    hi = x.astype(BF16)
    lo = (x - hi.astype(F32)).astype(BF16)
    return hi, lo


def _chunk_cumsum(x, tri2):
    hi, lo = _split2(x)
    return _dot(tri2, jnp.concatenate([hi, lo], axis=0))


def _tri2():
    r = lax.broadcasted_iota(jnp.int32, (CHUNK, 2 * CHUNK), 0)
    c = lax.broadcasted_iota(jnp.int32, (CHUNK, 2 * CHUNK), 1)
    c = jnp.where(c >= CHUNK, c - CHUNK, c)
    return jnp.where(r >= c, 1.0, 0.0).astype(BF16)


def _causal():
    r = lax.broadcasted_iota(jnp.int32, (CHUNK, CHUNK), 0)
    c = lax.broadcasted_iota(jnp.int32, (CHUNK, CHUNK), 1)
    return r >= c


def _block_kernel(
        x_ref, xp_ref, p_ref,
        nw_ref, ws_ref, up_ref, gb_ref, wx_ref, cw_ref, cb_ref, wq_ref, wk_ref, wv_ref, wg_ref, wz_ref,
        gnw_ref, dtb_ref, alog_ref, e2_ref, dskip_ref, snw_ref, wo_ref, wpe_ref, wgate_ref, pnw_ref,
        fnw_ref,
        o_ref,
        s_buf, xs_buf, bc_buf, qd_buf, ki_buf, kd_buf, dl_buf, v_buf, sg_buf, sz_buf,
        tail_ref, gst_ref, sst_ref, mix_ref, *, final):
    rows = x_ref.shape[0]
    chunks = range(rows // CHUNK)
    rsl = lambda c: slice(c * CHUNK, (c + 1) * CHUNK)
    j = pl.program_id(1)
    cur = j % 2
    prev = 1 - cur

    @pl.when(j == 0)
    def _():
        tail_ref[...] = jnp.zeros_like(tail_ref)
        gst_ref[...] = jnp.zeros_like(gst_ref)
        sst_ref[...] = jnp.zeros_like(sst_ref)
        for buf in (s_buf, xs_buf, bc_buf, qd_buf, ki_buf, kd_buf, dl_buf, v_buf, sg_buf, sz_buf):
            buf[1] = jnp.zeros(buf.shape[1:], buf.dtype)


    groups = range(SSD_GROUPS)
    gsl = lambda g: slice(g * SSD_GROUP_WIDTH, (g + 1) * SSD_GROUP_WIDTH)
    lane = lax.broadcasted_iota(jnp.int32, (1, SMALL_WIDTH), 1)
    is_dt = (lane >= DT_LANE0) & (lane < DT_LANE0 + SSD_HEADS)
    dt = jnp.where(is_dt, jax.nn.softplus(s_buf[prev] + dtb_ref[...]), 0.0)
    da = dt * jnp.where(is_dt, -jnp.exp(alog_ref[...]), 0.0)
    e2 = e2_ref[...]

    def expand(a):
        hi, lo = _split2(a)
        return _dot(jnp.concatenate([hi, lo], axis=1), e2)

    tri2 = _tri2()
    cum_lanes = _chunk_cumsum(jnp.concatenate([da[rsl(c)] for c in chunks], axis=1), tri2)
    cum = [cum_lanes[:, c * SMALL_WIDTH:(c + 1) * SMALL_WIDTH] for c in chunks]
    cum_e = expand(jnp.concatenate(cum, axis=0))
    dt_e = expand(dt)

    u = _rmsnorm(x_ref[...], nw_ref[...]).astype(BF16)
    small = _dot(u, ws_ref[...])
    s_buf[cur] = small
    gate = _dot(small.astype(BF16), up_ref[...]) + gb_ref[...]
    xr = _dot(u, wx_ref[...])

    pair_rows = []
    for c in chunks:
        cum_t = cum[c].T
        pair_rows.append(jnp.concatenate(
            [cum_t[DT_LANE0:DT_LANE0 + SSD_HEADS], cum_t[DT_LANE0 + 1:DT_LANE0 + SSD_HEADS + 1]], axis=1))
    xs = xs_buf[prev]
    xdt = xs * dt_e
    xdt_b = xdt.astype(BF16)
    decay_in = jnp.exp(cum_e)
    cum_last = [cum_e[(c + 1) * CHUNK - 1:(c + 1) * CHUNK, :] for c in chunks]
    decay_last = [jnp.exp(cl) for cl in cum_last]
    xd_end = [(xdt[rsl(c)] * jnp.exp(cum_last[c] - cum_e[rsl(c)])).astype(BF16) for c in chunks]
    bsl = lambda g: slice(g * SSD_STATE, (g + 1) * SSD_STATE)
    csl = lambda g: slice(SSD_BC_WIDTH + g * SSD_STATE, SSD_BC_WIDTH + (g + 1) * SSD_STATE)

    q = _dot(u, wq_ref[...]) * (GLA_HEAD_K ** -0.5)
    k = _dot(u, wk_ref[...])

    causal = _causal()
    heads = range(GLA_HEADS)
    ksl = lambda h: slice(h * GLA_HEAD_K, (h + 1) * GLA_HEAD_K)
    vsl = lambda h: slice(h * GLA_HEAD_V, (h + 1) * GLA_HEAD_V)
    scores = [[jnp.where(causal, _dot_nt(qd_buf[prev, rsl(c), ksl(h)], ki_buf[prev, rsl(c), ksl(h)]),
                         0.0).astype(BF16) for c in chunks] for h in heads]
    g_upd = [[_dot_tn(v_buf[prev, rsl(c), vsl(h)], kd_buf[prev, rsl(c), ksl(h)]) for c in chunks]
             for h in heads]
    quad = 4 * SSD_HEAD_DIM
    r4 = lax.broadcasted_iota(jnp.int32, (CHUNK, quad), 0)
    l4 = lax.broadcasted_iota(jnp.int32, (CHUNK, quad), 1)
    causal4 = (l4 & (SSD_HEAD_DIM - 1)) <= r4
    rb = lax.broadcasted_iota(jnp.int32, (quad, quad), 0)
    lb = lax.broadcasted_iota(jnp.int32, (quad, quad), 1)
    block_diag = (rb // SSD_HEAD_DIM) == (lb // SSD_HEAD_DIM)
    cb4 = [[_dot_nt(bc_buf[prev, rsl(c), csl(g)],
                    jnp.concatenate([bc_buf[prev, rsl(c), bsl(g)]] * 4, axis=0))
            for g in groups] for c in chunks]
    s_upd = [[_dot_tn(bc_buf[prev, rsl(c), bsl(g)], xd_end[c][:, gsl(g)]) for g in groups]
             for c in chunks]

    row8 = lax.broadcasted_iota(jnp.int32, (SUBLANES, 1), 0)

    def shift_rows(a, prev_tile, s):
        rolled = pltpu.roll(a, s, 0)
        head = jnp.where(row8 < s, pltpu.roll(prev_tile, s, 0), rolled[:SUBLANES])
        return jnp.concatenate([head, rolled[SUBLANES:]], axis=0)

    x1 = shift_rows(xr, tail_ref[0], 1)
    pair_lo = xr * cw_ref[1:2, :] + x1 * cw_ref[0:1, :]
    acc = cb_ref[...] + (xr * cw_ref[3:4, :] + x1 * cw_ref[2:3, :]) + shift_rows(pair_lo, tail_ref[1], 2)
    tail_ref[0] = xr[rows - SUBLANES:]
    tail_ref[1] = pair_lo[rows - SUBLANES:]
    xbc = jax.nn.silu(acc)
    xs_buf[cur] = xbc[:, :SSD_WIDTH]
    bc_buf[cur] = xbc[:, SSD_WIDTH:].astype(BF16)
    v_buf[cur] = _dot(u, wv_ref[...]).astype(BF16)

    intra = []
    for c in chunks:
        parts = []
        for g in groups:
            for qd in range(SSD_HEADS_PER_GROUP // 4):
                h0 = g * SSD_HEADS_PER_GROUP + qd * 4
                ls = slice(h0 * SSD_HEAD_DIM, h0 * SSD_HEAD_DIM + quad)
                row = jnp.concatenate([pair_rows[c][h0:h0 + 1], pair_rows[c][h0 + 2:h0 + 3]], axis=1)
                seg = cum_e[rsl(c), ls] - row
                l_q = jnp.where(causal4, cb4[c][g] * jnp.exp(jnp.minimum(seg, 0.0)), 0.0).astype(BF16)
                x_q = xdt_b[rsl(c), ls]
                x_bd = jnp.where(block_diag, jnp.concatenate([x_q] * 4, axis=0), jnp.zeros((), BF16))
                parts.append(_dot(l_q, x_bd))
        intra.append(jnp.concatenate(parts, axis=1))

    g_states = []
    for h in heads:
        state = gst_ref[h]
        per_chunk = []
        for c in chunks:
            per_chunk.append(state.astype(BF16))
            state = state * dl_buf[prev, c:c + 1, ksl(h)] + g_upd[h][c]
        gst_ref[h] = state
        g_states.append(per_chunk)

    state = [sst_ref[g] for g in groups]
    s_states = []
    for c in chunks:
        s_states.append([s.astype(BF16) for s in state])
        state = [state[g] * decay_last[c][:, gsl(g)] + s_upd[c][g] for g in groups]
    for g in groups:
        sst_ref[g] = state[g]

    log2_a = jax.nn.log_sigmoid(gate) * (LOG2_E / GLA_GATE_NORMALIZER)
    b_all = _chunk_cumsum(jnp.concatenate([log2_a[rsl(c)] for c in chunks], axis=1), tri2)
    sg_buf[cur] = jax.nn.silu(_dot(u, wg_ref[...]))

    gnw = gnw_ref[...]
    for h in heads:
        for c in chunks:
            o = (_dot(scores[h][c], v_buf[prev, rsl(c), vsl(h)])
                 + _dot_nt(qd_buf[prev, rsl(c), ksl(h)], g_states[h][c]))
            mix_ref[rsl(c), vsl(h)] = (_rmsnorm(o, gnw) * sg_buf[prev, rsl(c), vsl(h)]).astype(BF16)
    dskip = dskip_ref[...]
    snw = snw_ref[...]
    for c in chunks:
        inter = jnp.concatenate([_dot(bc_buf[prev, rsl(c), csl(g)], s_states[c][g]) for g in groups],
                                axis=1)
        y = intra[c] + inter * decay_in[rsl(c)] + xs[rsl(c)] * dskip
        y = y * sz_buf[prev, rsl(c), :]
        outs = [_rmsnorm(y[:, gsl(g)], snw[:, gsl(g)]) for g in groups]
        mix_ref[rsl(c), GLA_WIDTH:] = jnp.concatenate(outs, axis=1).astype(BF16)

    sz_buf[cur] = jax.nn.silu(_dot(u, wz_ref[...]))
    dl_buf[cur] = jnp.zeros(dl_buf.shape[1:], F32)
    for c in chunks:
        b = b_all[:, c * GLA_KEY_WIDTH:(c + 1) * GLA_KEY_WIDTH]
        b_last = b[CHUNK - 1:CHUNK, :]
        qd_buf[cur, rsl(c), :] = (q[rsl(c)] * jnp.exp2(b)).astype(BF16)
        ki_buf[cur, rsl(c), :] = (k[rsl(c)] * jnp.exp2(-b)).astype(BF16)
        kd_buf[cur, rsl(c), :] = (k[rsl(c)] * jnp.exp2(b_last - b)).astype(BF16)
        dl_buf[cur, c:c + 1, :] = jnp.exp2(b_last)

    hres = xp_ref[...] + _dot(mix_ref[...], wo_ref[...])
    hn = _rmsnorm(hres, pnw_ref[...]).astype(BF16)
    pe_gate = jax.nn.sigmoid(_dot(hn, wgate_ref[...]))
    hres = hres + pe_gate * _dot(p_ref[...].astype(BF16), wpe_ref[...])
    if final:
        hres = _rmsnorm(hres, fnw_ref[...])
    o_ref[...] = hres


def _block_call(x3, p3, final, stage1_params, stage2_params):
    bsz, t, d = x3.shape
    tb = BLOCK_ROWS
    nblk = t // tb
    grid = (bsz, nblk + 1)
    cur_blk = lambda w: pl.BlockSpec((None, tb, w), lambda b, j: (b, jnp.minimum(j, nblk - 1), 0))
    prev_blk = lambda w: pl.BlockSpec((None, tb, w), lambda b, j: (b, jnp.maximum(j - 1, 0), 0))
    fixed = lambda a: pl.BlockSpec(a.shape, lambda b, j: (0,) * a.ndim, pipeline_mode=pl.Buffered(1))
    params = tuple(stage1_params) + tuple(stage2_params)
    slots = lambda w, dt, r=tb: pltpu.VMEM((2, r, w), dt)
    scratch = [
        slots(SMALL_WIDTH, F32), slots(SSD_WIDTH, F32), slots(2 * SSD_BC_WIDTH, BF16),
        slots(GLA_KEY_WIDTH, BF16), slots(GLA_KEY_WIDTH, BF16), slots(GLA_KEY_WIDTH, BF16),
        slots(GLA_KEY_WIDTH, F32, SUBLANES), slots(GLA_WIDTH, BF16), slots(GLA_WIDTH, F32),
        slots(SSD_WIDTH, F32),
        pltpu.VMEM((2, SUBLANES, SSD_CONV_DIM), F32),
        pltpu.VMEM((GLA_HEADS, GLA_HEAD_V, GLA_HEAD_K), F32),
        pltpu.VMEM((SSD_GROUPS, SSD_STATE, SSD_GROUP_WIDTH), F32),
        pltpu.VMEM((tb, D_MIX), BF16),
    ]
    return pl.pallas_call(
        functools.partial(_block_kernel, final=final),
        grid=grid,
        in_specs=[cur_blk(d), prev_blk(d), prev_blk(p3.shape[2])] + [fixed(a) for a in params],
        out_specs=prev_blk(d),
        out_shape=jax.ShapeDtypeStruct((bsz, t, d), F32),
        scratch_shapes=scratch,
        compiler_params=pltpu.CompilerParams(
            dimension_semantics=("arbitrary", "arbitrary"), vmem_limit_bytes=VMEM_LIMIT),
        name="hybrid_block",
    )(x3, x3, p3, *params)


def _expansion_matrix():
    e = np.zeros((2 * SMALL_WIDTH, SSD_WIDTH), np.float32)
    for h in range(SSD_HEADS):
        e[DT_LANE0 + h, h * SSD_HEAD_DIM:(h + 1) * SSD_HEAD_DIM] = 1.0
        e[SMALL_WIDTH + DT_LANE0 + h, h * SSD_HEAD_DIM:(h + 1) * SSD_HEAD_DIM] = 1.0
    return jnp.asarray(e, BF16)


def _pad_small(v):
    return jnp.zeros((1, SMALL_WIDTH), F32).at[0, DT_LANE0:DT_LANE0 + SSD_HEADS].set(v.astype(F32))


def _layer(x3, p3, final, norm_w, w_in, gla_gate_up, gla_gate_b, gla_norm_w, conv_w, conv_b,
           dt_bias, a_log, d_skip, ssd_norm_w, w_out, w_pe, w_pe_gate, pe_norm_w, final_norm_w):
    d = x3.shape[2]
    sizes = (GLA_KEY_WIDTH, GLA_KEY_WIDTH, GLA_WIDTH, GLA_WIDTH, GLA_GATE_RANK, SSD_WIDTH,
             SSD_CONV_DIM, SSD_HEADS)
    offs = np.concatenate([[0], np.cumsum(sizes)])
    wq, wk, wv, wg, wlr, wz, wx, wdt = [w_in[:, offs[i]:offs[i + 1]] for i in range(len(sizes))]
    pad = jnp.zeros((d, SMALL_WIDTH - GLA_GATE_RANK - SSD_HEADS), w_in.dtype)
    wsmall = jnp.concatenate([wlr, wdt, pad], axis=1)
    wq, wk, wv, wg, wz, wx, wsmall = [w.astype(BF16) for w in (wq, wk, wv, wg, wz, wx, wsmall)]
    up_pad = jnp.zeros((SMALL_WIDTH, GLA_KEY_WIDTH), BF16).at[:GLA_GATE_RANK].set(gla_gate_up.astype(BF16))
    row = lambda a: a.reshape(1, -1).astype(F32)

    stage1 = (row(norm_w), wsmall, up_pad, row(gla_gate_b), wx, conv_w.astype(F32), row(conv_b),
              wq, wk, wv, wg, wz)
    stage2 = (row(gla_norm_w), _pad_small(dt_bias), _pad_small(a_log), _expansion_matrix(),
              row(jnp.repeat(d_skip, SSD_HEAD_DIM)), row(ssd_norm_w), w_out.astype(BF16),
              w_pe.astype(BF16), w_pe_gate.astype(BF16), row(pe_norm_w), row(final_norm_w))
    return _block_call(x3, p3, final, stage1, stage2)


def kernel(x, p, norm_w, w_in, gla_gate_up, gla_gate_b, gla_norm_w, conv_w, conv_b, dt_bias, a_log,
           d_skip, ssd_norm_w, w_out, w_pe, w_pe_gate, pe_norm_w, final_norm_w):
    depth = p.shape[0]
    h = x
    for i in range(depth):
        h = _layer(h, p[i], i == depth - 1,
                   norm_w[i], w_in[i], gla_gate_up[i], gla_gate_b[i], gla_norm_w[i], conv_w[i],
                   conv_b[i], dt_bias[i], a_log[i], d_skip[i], ssd_norm_w[i], w_out[i], w_pe[i],
                   w_pe_gate[i], pe_norm_w[i], final_norm_w)
    return h
```
